```python
import jax, jax.numpy as jnp
from jax import lax
import numpy as np

D_MODEL = 1024
BATCH = 8
SEQ = 8192
DEPTH = 4
DEC_BATCH = 8
DEC_SEQ = 32
PAST_LEN = 1024

CHUNK = 64
N_MIXERS = 2
N_ATTN_LAYERS = (DEPTH + 1) // 2
N_REC_LAYERS = DEPTH // 2
FOX_HEADS = 16
FOX_HEAD_DIM = D_MODEL // FOX_HEADS
FOX_SCALE = FOX_HEAD_DIM ** -0.5
HGRN_HEADS = 8
HGRN_KEY_DIM = D_MODEL // HGRN_HEADS
HGRN_VAL_DIM = D_MODEL // HGRN_HEADS
D_FF = 4 * D_MODEL
Q_BLOCK = 128
EPS = 1e-6
FOX_FORGET_BIAS = 2.0

kernel_name = "fox_hgrn2_streaming_step"


def rmsnorm(x, g):
    xf = x.astype(jnp.float32)
    y = xf * lax.rsqrt(jnp.mean(xf * xf, axis=-1, keepdims=True) + EPS)
    return (y * g.astype(jnp.float32)).astype(x.dtype)


def fox_attend(q, k, v, cq, ck, qpos, kpos):
    logits = jnp.einsum('bqhd,bkhd->bhqk', q, k).astype(jnp.float32) * FOX_SCALE
    bias = jnp.swapaxes(cq, 1, 2)[..., :, None] - jnp.swapaxes(ck, 1, 2)[..., None, :]
    mask = kpos[None, :] <= qpos[:, None]
    logits = jnp.where(mask, logits + bias, -jnp.inf)
    p = jax.nn.softmax(logits, axis=-1).astype(v.dtype)
    return jnp.einsum('bhqk,bkhd->bqhd', p, v)


def fox_block_sweep(q, k, v, c):
    B, T, H, Dh = q.shape
    nb = T // Q_BLOCK
    qb = q.reshape(B, nb, Q_BLOCK, H, Dh).transpose(1, 0, 2, 3, 4)
    cb = c.reshape(B, nb, Q_BLOCK, H).transpose(1, 0, 2, 3)
    pb = jnp.arange(T).reshape(nb, Q_BLOCK)
    kpos = jnp.arange(T)

    def one_block(args):
        qi, ci, pi = args
        return fox_attend(qi, k, v, ci, c, pi, kpos)

    out = lax.map(one_block, (qb, cb, pb))
    return out.transpose(1, 0, 2, 3, 4).reshape(B, T, H, Dh)


def fox_mixer(h, w_in, b_f, g_q, g_k, w_out, past_k, past_v, past_logf):
    B, T, _ = h.shape
    proj = h @ w_in
    q = proj[..., :D_MODEL].reshape(B, T, FOX_HEADS, FOX_HEAD_DIM)
    k = proj[..., D_MODEL:2 * D_MODEL].reshape(B, T, FOX_HEADS, FOX_HEAD_DIM)
    v = proj[..., 2 * D_MODEL:3 * D_MODEL].reshape(B, T, FOX_HEADS, FOX_HEAD_DIM)
    gate = proj[..., 3 * D_MODEL:4 * D_MODEL]
    fz = proj[..., 4 * D_MODEL:]
    q = rmsnorm(q, g_q)
    k = rmsnorm(k, g_k)
    logf = jax.nn.log_sigmoid((fz + b_f).astype(jnp.float32))
    if past_k is None:
        c = jnp.cumsum(logf, axis=1)
        o = fox_block_sweep(q, k, v, c)
    else:
        P = past_k.shape[1]
        k_all = jnp.concatenate([past_k, k], axis=1)
        v_all = jnp.concatenate([past_v, v], axis=1)
        c = jnp.cumsum(jnp.concatenate([past_logf.astype(jnp.float32), logf], axis=1), axis=1)
        o = fox_attend(q, k_all, v_all, c[:, P:], c, P + jnp.arange(T), jnp.arange(P + T))
    o = o.reshape(B, T, D_MODEL) * jax.nn.sigmoid(gate)
    return o @ w_out, k, v, logf.astype(h.dtype)


def hgrn2_chunkwise(q, k, v, logf, s0):
    B, T, H, DK = q.shape
    DV = v.shape[-1]
    L = CHUNK if T % CHUNK == 0 else T
    n = T // L

    def to_chunks(a):
        return a.reshape(B, n, L, H, a.shape[-1]).transpose(1, 0, 3, 2, 4)

    causal = jnp.tril(jnp.ones((L, L), dtype=bool))

    def step(S, inp):
        qc, kc, vc, gc = inp
        b = jnp.cumsum(gc, axis=2)
        o_inter = jnp.einsum('bhtk,bhkv->bhtv', qc * jnp.exp(b), S)
        diff = b[:, :, :, None, :] - b[:, :, None, :, :]
        decay = jnp.exp(jnp.where(causal[:, :, None], diff, -jnp.inf))
        A = jnp.einsum('bhtk,bhsk,bhtsk->bhts', qc, kc, decay)
        o_intra = jnp.einsum('bhts,bhsv->bhtv', A, vc)
        bl = b[:, :, -1, :]
        S_new = jnp.exp(bl)[..., None] * S + jnp.einsum(
            'bhsk,bhsv->bhkv', kc * jnp.exp(bl[:, :, None, :] - b), vc)
        return S_new, o_inter + o_intra

    S, o = lax.scan(step, s0, (to_chunks(q), to_chunks(k), to_chunks(v), to_chunks(logf)))
    o = o.transpose(1, 0, 3, 2, 4).reshape(B, T, H, DV)
    return o, S


def hgrn2_mixer(h, w_in, lb, g_out, w_out, s0):
    B, T, _ = h.shape
    proj = (h @ w_in).astype(jnp.float32)
    shp_k = (B, T, HGRN_HEADS, HGRN_KEY_DIM)
    q = jax.nn.silu(proj[..., :D_MODEL]).reshape(shp_k)
    f = lb + (1.0 - lb) * jax.nn.sigmoid(proj[..., D_MODEL:2 * D_MODEL])
    f = f.reshape(shp_k)
    i = proj[..., 2 * D_MODEL:3 * D_MODEL].reshape(B, T, HGRN_HEADS, HGRN_VAL_DIM)
    gate = proj[..., 3 * D_MODEL:]
    o, S = hgrn2_chunkwise(q, 1.0 - f, i, jnp.log(f), s0.astype(jnp.float32))
    o = rmsnorm(o.reshape(B, T, D_MODEL), g_out) * jax.nn.silu(gate)
    return o.astype(h.dtype) @ w_out, S.astype(h.dtype)


def trunk(x, past_k, past_v, past_logf, past_s,
          fox_w_in, fox_b_f, fox_q_norm, fox_k_norm, fox_w_out,
          hgrn_w_in, hgrn_lb_logits, hgrn_out_norm, hgrn_w_out,
          pre_mix_norm, post_mix_norm, pre_ffn_norm, post_ffn_norm, ffn_w_up, ffn_w_down):
    B = x.shape[0]
    sm = jax.nn.softmax(hgrn_lb_logits.astype(jnp.float32), axis=0)
    lower_bounds = jnp.cumsum(sm, axis=0) - sm[0]
    new_k, new_v, new_logf, new_s = [], [], [], []
    for layer in range(DEPTH):
        j = layer // N_MIXERS
        h = rmsnorm(x, pre_mix_norm[layer])
        if layer % N_MIXERS == 0:
            pk = None if past_k is None else past_k[j]
            pv = None if past_v is None else past_v[j]
            pl = None if past_logf is None else past_logf[j]
            out, k, v, lf = fox_mixer(h, fox_w_in[j], fox_b_f[j], fox_q_norm[j], fox_k_norm[j],
                                      fox_w_out[j], pk, pv, pl)
            new_k.append(k)
            new_v.append(v)
            new_logf.append(lf)
        else:
            s0 = (jnp.zeros((B, HGRN_HEADS, HGRN_KEY_DIM, HGRN_VAL_DIM), jnp.float32)
                  if past_s is None else past_s[j])
            out, s = hgrn2_mixer(h, hgrn_w_in[j], lower_bounds[layer], hgrn_out_norm[j],
                                 hgrn_w_out[j], s0)
            new_s.append(s)
        x = x + rmsnorm(out, post_mix_norm[layer])
        h = rmsnorm(x, pre_ffn_norm[layer])
        u = jax.nn.relu(h @ ffn_w_up[layer])
        x = x + rmsnorm((u * u) @ ffn_w_down[layer], post_ffn_norm[layer])
    return x, jnp.stack(new_k), jnp.stack(new_v), jnp.stack(new_logf), jnp.stack(new_s)


def setup_inputs(seed: int = 0) -> dict:
    key = jax.random.key(seed)
    ks = jax.random.split(key, 24)
    f32 = jnp.float32
    nrm = lambda k, shp, s: jax.random.normal(k, shp, f32) * s
    gain = lambda k, shp: 1.0 + 0.05 * jax.random.normal(k, shp, f32)
    d = D_MODEL
    return {
        "x_prompt": nrm(ks[0], (BATCH, SEQ, d), 1.0),
        "x_sample": nrm(ks[1], (DEC_BATCH, DEC_SEQ, d), 1.0),
        "cache_k": nrm(ks[2], (N_ATTN_LAYERS, DEC_BATCH, PAST_LEN, FOX_HEADS, FOX_HEAD_DIM), 1.0),
        "cache_v": nrm(ks[3], (N_ATTN_LAYERS, DEC_BATCH, PAST_LEN, FOX_HEADS, FOX_HEAD_DIM), 1.0),
        "cache_logf": jax.nn.log_sigmoid(FOX_FORGET_BIAS + nrm(ks[4], (N_ATTN_LAYERS, DEC_BATCH, PAST_LEN, FOX_HEADS), 0.5)),
        "state_s": nrm(ks[5], (N_REC_LAYERS, DEC_BATCH, HGRN_HEADS, HGRN_KEY_DIM, HGRN_VAL_DIM), 0.5),
        "fox_w_in": nrm(ks[6], (N_ATTN_LAYERS, d, 4 * d + FOX_HEADS), d ** -0.5),
        "fox_b_f": FOX_FORGET_BIAS + nrm(ks[7], (N_ATTN_LAYERS, FOX_HEADS), 0.5),
        "fox_q_norm": gain(ks[8], (N_ATTN_LAYERS, FOX_HEAD_DIM)),
        "fox_k_norm": gain(ks[9], (N_ATTN_LAYERS, FOX_HEAD_DIM)),
        "fox_w_out": nrm(ks[10], (N_ATTN_LAYERS, d, d), d ** -0.5),
        "hgrn_w_in": nrm(ks[11], (N_REC_LAYERS, d, 4 * d), d ** -0.5),
        "hgrn_lb_logits": nrm(ks[12], (DEPTH, d), 0.5),
        "hgrn_out_norm": gain(ks[13], (N_REC_LAYERS, d)),
        "hgrn_w_out": nrm(ks[14], (N_REC_LAYERS, d, d), d ** -0.5),
        "pre_mix_norm": gain(ks[15], (DEPTH, d)),
        "post_mix_norm": gain(ks[16], (DEPTH, d)),
        "pre_ffn_norm": gain(ks[17], (DEPTH, d)),
        "post_ffn_norm": gain(ks[18], (DEPTH, d)),
        "ffn_w_up": nrm(ks[19], (DEPTH, d, D_FF), d ** -0.5),
        "ffn_w_down": nrm(ks[20], (DEPTH, D_FF, d), D_FF ** -0.5),
    }


def reference(x_prompt, x_sample, cache_k, cache_v, cache_logf, state_s,
              fox_w_in, fox_b_f, fox_q_norm, fox_k_norm, fox_w_out,
              hgrn_w_in, hgrn_lb_logits, hgrn_out_norm, hgrn_w_out,
              pre_mix_norm, post_mix_norm, pre_ffn_norm, post_ffn_norm, ffn_w_up, ffn_w_down):
    y_prompt, k_p, v_p, lf_p, s_p = trunk(
        x_prompt, None, None, None, None,
        fox_w_in, fox_b_f, fox_q_norm, fox_k_norm, fox_w_out,
        hgrn_w_in, hgrn_lb_logits, hgrn_out_norm, hgrn_w_out,
        pre_mix_norm, post_mix_norm, pre_ffn_norm, post_ffn_norm, ffn_w_up, ffn_w_down)
    y_sample, k_s, v_s, lf_s, s_s = trunk(
        x_sample, cache_k, cache_v, cache_logf, state_s,
        fox_w_in, fox_b_f, fox_q_norm, fox_k_norm, fox_w_out,
        hgrn_w_in, hgrn_lb_logits, hgrn_out_norm, hgrn_w_out,
        pre_mix_norm, post_mix_norm, pre_ffn_norm, post_ffn_norm, ffn_w_up, ffn_w_down)
    return (y_prompt, y_sample, k_p, v_p, lf_p, s_p, k_s, v_s, lf_s, s_s)
```

```python
import functools

import jax
import jax.numpy as jnp
from jax import lax
from jax.experimental import pallas as pl
from jax.experimental.pallas import tpu as pltpu

D_MODEL = 1024
DEPTH = 4
FOX_HEADS = 16
FOX_HEAD_DIM = D_MODEL // FOX_HEADS
FOX_SCALE = FOX_HEAD_DIM ** -0.5
HGRN_HEADS = 8
HGRN_KEY_DIM = D_MODEL // HGRN_HEADS
D_FF = 4 * D_MODEL
EPS = 1e-6

LANES = 128
HEAD_PAIRS = FOX_HEADS // 2
NEG_BIG = -1e30
VMEM_LIMIT = 56 * 1024 * 1024

F32 = jnp.float32
BF16 = jnp.bfloat16


def _cparams(*sem):
    return pltpu.CompilerParams(dimension_semantics=sem, vmem_limit_bytes=VMEM_LIMIT)


def _row_tile(m, cap):
    t = min(m, cap)
    assert m % t == 0, (m, t)
    return t


def _resident(shape):
    nd = len(shape)
    return pl.BlockSpec(shape, lambda *_: (0,) * nd, pipeline_mode=pl.Buffered(1))


def _rms(x, g):
    return x * lax.rsqrt(jnp.mean(x * x, axis=-1, keepdims=True) + EPS) * g


def _sigmoid(z):
    return 1.0 / (1.0 + jnp.exp(-z))


def _split3(x):
    hi = x.astype(BF16)
    r = x - hi.astype(F32)
    mid = r.astype(BF16)
    lo = (r - mid.astype(F32)).astype(BF16)
    return hi, mid, lo


def _fox_inproj_kernel(x_ref, gpre_ref, w_ref, wf_ref, bf_ref, gq_ref, gk_ref, sel_ref, selt_ref,
                       q_ref, k32_ref, k16_ref, v32_ref, v16_ref, gate_ref, logf_ref, logfp_ref):
    h = _rms(x_ref[...], gpre_ref[...]).astype(BF16)

    def head_norm(z, g):
        ssq = jnp.dot((z * z).astype(BF16), sel_ref[...], preferred_element_type=F32)
        inv = lax.rsqrt(ssq * (1.0 / FOX_HEAD_DIM) + EPS)
        hi = inv.astype(BF16)
        lo = (inv - hi.astype(F32)).astype(BF16)
        full = (jnp.dot(hi, selt_ref[...], preferred_element_type=F32)
                + jnp.dot(lo, selt_ref[...], preferred_element_type=F32))
        return z * full * g

    zq = jnp.dot(h, w_ref[:, 0:D_MODEL], preferred_element_type=F32)
    q_ref[...] = (head_norm(zq, gq_ref[...]) * FOX_SCALE).astype(BF16)
    zk = jnp.dot(h, w_ref[:, D_MODEL:2 * D_MODEL], preferred_element_type=F32)
    kn = head_norm(zk, gk_ref[...])
    k32_ref[...] = kn
    k16_ref[...] = kn.astype(BF16)
    zv = jnp.dot(h, w_ref[:, 2 * D_MODEL:3 * D_MODEL], preferred_element_type=F32)
    v32_ref[...] = zv
    v16_ref[...] = zv.astype(BF16)
    gate_ref[...] = jnp.dot(h, w_ref[:, 3 * D_MODEL:4 * D_MODEL], preferred_element_type=F32).astype(BF16)
    z = jnp.dot(h, wf_ref[...], preferred_element_type=F32) + bf_ref[...]
    logf = jnp.minimum(z, 0.0) - jnp.log(1.0 + jnp.exp(-jnp.abs(z)))
    logfp_ref[...] = logf
    logf_ref[...] = logf[:, :FOX_HEADS]


def _fox_inproj(x2, gpre, w_main, w_f, b_f, gq, gk, sel, selt):
    m = x2.shape[0]
    tm = _row_tile(m, 512)
    row = lambda n: pl.BlockSpec((tm, n), lambda i: (i, 0))
    wide = lambda dt: jax.ShapeDtypeStruct((m, D_MODEL), dt)
    return pl.pallas_call(
        _fox_inproj_kernel,
        grid=(m // tm,),
        in_specs=[row(D_MODEL), _resident((1, D_MODEL)), _resident(w_main.shape), _resident(w_f.shape),
                  _resident((1, LANES)), _resident((1, D_MODEL)), _resident((1, D_MODEL)),
                  _resident(sel.shape), _resident(selt.shape)],
        out_specs=[row(D_MODEL)] * 6 + [row(FOX_HEADS), row(LANES)],
        out_shape=[wide(BF16), wide(F32), wide(BF16), wide(F32), wide(BF16), wide(BF16),
                   jax.ShapeDtypeStruct((m, FOX_HEADS), F32), jax.ShapeDtypeStruct((m, LANES), F32)],
        compiler_params=_cparams("parallel"),
        name="fox_inproj",
    )(x2, gpre, w_main, w_f, b_f, gq, gk, sel, selt)


def _cumsum_kernel(x_ref, tri_ref, o_ref, carry_ref):
    @pl.when(pl.program_id(1) == 0)
    def _():
        carry_ref[...] = jnp.zeros_like(carry_ref)

    hi, mid, lo = _split3(x_ref[0])
    parts = jnp.dot(tri_ref[...], jnp.concatenate([hi, mid, lo], axis=1), preferred_element_type=F32)
    c = (parts[:, :LANES] + parts[:, LANES:2 * LANES] + parts[:, 2 * LANES:]) + carry_ref[...]
    o_ref[0] = c
    carry_ref[...] = c[-1:, :]


def _cumsum_time(x):
    b, t, _ = x.shape
    tt = _row_tile(t, 512)
    tri = jnp.tril(jnp.ones((tt, tt), F32)).astype(BF16)
    blk = pl.BlockSpec((1, tt, LANES), lambda bi, ti: (bi, ti, 0))
    return pl.pallas_call(
        _cumsum_kernel,
        grid=(b, t // tt),
        in_specs=[blk, _resident((tt, tt))],
        out_specs=blk,
        out_shape=jax.ShapeDtypeStruct(x.shape, F32),
        scratch_shapes=[pltpu.VMEM((1, LANES), F32)],
        compiler_params=_cparams("parallel", "arbitrary"),
        name="cumsum_time",
    )(x, tri)


def _fox_attn_kernel(q_ref, k_ref, v_ref, g_ref, ct_ref, o_ref, m_sc, l_sc, acc_sc, *, tq):
    i = pl.program_id(2)
    lane = lax.broadcasted_iota(jnp.int32, (tq, LANES), 1)
    first = lane < FOX_HEAD_DIM
    q = q_ref[0]
    zero = jnp.zeros_like(q)
    qh = (jnp.where(first, q, zero), jnp.where(first, zero, q))
    m_sc[...] = jnp.full(m_sc.shape, NEG_BIG, F32)
    l_sc[...] = jnp.zeros(l_sc.shape, F32)
    acc_sc[...] = jnp.zeros(acc_sc.shape, F32)
    causal = (lax.broadcasted_iota(jnp.int32, (tq, tq), 0) >= lax.broadcasted_iota(jnp.int32, (tq, tq), 1))

    def block(j, diagonal):
        start = pl.multiple_of(j * tq, tq)
        kb = k_ref[0, pl.ds(start, tq), :]
        vb = v_ref[0, pl.ds(start, tq), :]
        for hh in range(2):
            s = lax.dot_general(qh[hh], kb, (((1,), (1,)), ((), ())), preferred_element_type=F32)
            s = s - ct_ref[0, 0, hh:hh + 1, pl.ds(start, tq)]
            if diagonal:
                s = jnp.where(causal, s, -jnp.inf)
            m_old = m_sc[hh]
            m_new = jnp.maximum(m_old, jnp.max(s, axis=-1, keepdims=True))
            p = jnp.exp(s - m_new)
            alpha = jnp.exp(m_old - m_new)
            l_sc[hh] = alpha * l_sc[hh] + jnp.sum(p, axis=-1, keepdims=True)
            acc_sc[hh] = alpha * acc_sc[hh] + jnp.dot(p.astype(BF16), vb, preferred_element_type=F32)
            m_sc[hh] = m_new

    block(i, True)

    def body(jj, carry):
        block(i - 1 - jj, False)
        return carry

    lax.fori_loop(0, i, body, 0)
    o = jnp.where(first, acc_sc[0] / l_sc[0], acc_sc[1] / l_sc[1])
    o_ref[0] = (o * _sigmoid(g_ref[0].astype(F32))).astype(BF16)


def _fox_attention(q, k, v, gate, ct):
    b, t, _ = q.shape
    tq = _row_tile(t, 512)
    qblk = pl.BlockSpec((1, tq, LANES), lambda bi, hp, i: (bi, i, hp))
    kvblk = pl.BlockSpec((1, t, LANES), lambda bi, hp, i: (bi, 0, hp))
    return pl.pallas_call(
        functools.partial(_fox_attn_kernel, tq=tq),
        grid=(b, HEAD_PAIRS, t // tq),
        in_specs=[qblk, kvblk, kvblk, qblk, pl.BlockSpec((1, 1, 2, t), lambda bi, hp, i: (bi, hp, 0, 0))],
        out_specs=qblk,
        out_shape=jax.ShapeDtypeStruct(q.shape, BF16),
        scratch_shapes=[pltpu.VMEM((2, tq, 1), F32), pltpu.VMEM((2, tq, 1), F32),
                        pltpu.VMEM((2, tq, LANES), F32)],
        compiler_params=_cparams("parallel", "parallel", "arbitrary"),
        name="fox_attention",
    )(q, k, v, gate, ct)


def _fox_decode_kernel(q_ref, kp_ref, vp_ref, kn_ref, vn_ref, g_ref, ctp_ref, ctn_ref, o_ref):
    tn = q_ref.shape[1]
    lane = lax.broadcasted_iota(jnp.int32, (tn, LANES), 1)
    first = lane < FOX_HEAD_DIM
    q = q_ref[0]
    zero = jnp.zeros_like(q)
    kp = kp_ref[0].astype(BF16)
    vp = vp_ref[0].astype(BF16)
    kn = kn_ref[0]
    vn = vn_ref[0]
    causal = (lax.broadcasted_iota(jnp.int32, (tn, tn), 0) >= lax.broadcasted_iota(jnp.int32, (tn, tn), 1))
    nt = (((1,), (1,)), ((), ()))
    outs = []
    for hh in range(2):
        qh = jnp.where(first, q, zero) if hh == 0 else jnp.where(first, zero, q)
        sp = lax.dot_general(qh, kp, nt, preferred_element_type=F32) - ctp_ref[0, 0, hh:hh + 1, :]
        sn = lax.dot_general(qh, kn, nt, preferred_element_type=F32) - ctn_ref[0, 0, hh:hh + 1, :]
        sn = jnp.where(causal, sn, -jnp.inf)
        m = jnp.maximum(jnp.max(sp, axis=-1, keepdims=True), jnp.max(sn, axis=-1, keepdims=True))
        pp = jnp.exp(sp - m)
        pn = jnp.exp(sn - m)
        l = jnp.sum(pp, axis=-1, keepdims=True) + jnp.sum(pn, axis=-1, keepdims=True)
        acc = (jnp.dot(pp.astype(BF16), vp, preferred_element_type=F32)
               + jnp.dot(pn.astype(BF16), vn, preferred_element_type=F32))
        outs.append(acc / l)
    o = jnp.where(first, outs[0], outs[1])
    o_ref[0] = (o * _sigmoid(g_ref[0].astype(F32))).astype(BF16)


def _fox_decode_attention(q, kp, vp, kn, vn, gate, ctp, ctn):
    b, tn, _ = q.shape
    p = kp.shape[1]
    new = pl.BlockSpec((1, tn, LANES), lambda bi, hp: (bi, 0, hp))
    past = pl.BlockSpec((1, p, LANES), lambda bi, hp: (bi, 0, hp))
    return pl.pallas_call(
        _fox_decode_kernel,
        grid=(b, HEAD_PAIRS),
        in_specs=[new, past, past, new, new, new,
                  pl.BlockSpec((1, 1, 2, p), lambda bi, hp: (bi, hp, 0, 0)),
                  pl.BlockSpec((1, 1, 2, tn), lambda bi, hp: (bi, hp, 0, 0))],
        out_specs=new,
        out_shape=jax.ShapeDtypeStruct(q.shape, BF16),
        compiler_params=_cparams("parallel", "parallel"),
        name="fox_decode_attention",
    )(q, kp, vp, kn, vn, gate, ctp, ctn)


def _fox_outproj_kernel(a_ref, x_ref, w_ref, gpost_ref, o_ref):
    y = jnp.dot(a_ref[...], w_ref[...], preferred_element_type=F32)
    o_ref[...] = x_ref[...] + _rms(y, gpost_ref[...])


def _hgrn_outproj_kernel(a_ref, sg_ref, x_ref, gout_ref, w_ref, gpost_ref, o_ref):
    a = _rms(a_ref[...].astype(F32), gout_ref[...]) * sg_ref[...].astype(F32)
    y = jnp.dot(a.astype(BF16), w_ref[...], preferred_element_type=F32)
    o_ref[...] = x_ref[...] + _rms(y, gpost_ref[...])


def _fox_outproj(a, x2, w, gpost):
    m = x2.shape[0]
    tm = _row_tile(m, 512)
    row = pl.BlockSpec((tm, D_MODEL), lambda i: (i, 0))
    return pl.pallas_call(
        _fox_outproj_kernel,
        grid=(m // tm,),
        in_specs=[row, row, _resident(w.shape), _resident((1, D_MODEL))],
        out_specs=row,
        out_shape=jax.ShapeDtypeStruct(x2.shape, F32),
        compiler_params=_cparams("parallel"),
        name="fox_outproj",
    )(a, x2, w, gpost)


def _hgrn_outproj(a, sg, x2, gout, w, gpost):
    m = x2.shape[0]
    tm = _row_tile(m, 512)
    row = pl.BlockSpec((tm, D_MODEL), lambda i: (i, 0))
    return pl.pallas_call(
        _hgrn_outproj_kernel,
        grid=(m // tm,),
        in_specs=[row, row, row, _resident((1, D_MODEL)), _resident(w.shape), _resident((1, D_MODEL))],
        out_specs=row,
        out_shape=jax.ShapeDtypeStruct(x2.shape, F32),
        compiler_params=_cparams("parallel"),
        name="hgrn_outproj",
    )(a, sg, x2, gout, w, gpost)


def _ffn_kernel(x_ref, gpre_ref, wup_ref, wdown_ref, gpost_ref, o_ref, h_sc, acc_sc):
    j = pl.program_id(1)

    @pl.when(j == 0)
    def _():
        h_sc[...] = _rms(x_ref[...], gpre_ref[...]).astype(BF16)
        acc_sc[...] = jnp.zeros_like(acc_sc)

    u = jnp.maximum(jnp.dot(h_sc[...], wup_ref[...], preferred_element_type=F32), 0.0)
    acc_sc[...] += jnp.dot((u * u).astype(BF16), wdown_ref[...], preferred_element_type=F32)

    @pl.when(j == pl.num_programs(1) - 1)
    def _():
        o_ref[...] = x_ref[...] + _rms(acc_sc[...], gpost_ref[...])


def _ffn(x2, gpre, wup, wdown, gpost):
    m = x2.shape[0]
    tm = _row_tile(m, 1024)
    tf = 512
    row = pl.BlockSpec((tm, D_MODEL), lambda i, j: (i, 0))
    return pl.pallas_call(
        _ffn_kernel,
        grid=(m // tm, D_FF // tf),
        in_specs=[row, _resident((1, D_MODEL)),
                  pl.BlockSpec((D_MODEL, tf), lambda i, j: (0, j)),
                  pl.BlockSpec((tf, D_MODEL), lambda i, j: (j, 0)),
                  _resident((1, D_MODEL))],
        out_specs=row,
        out_shape=jax.ShapeDtypeStruct(x2.shape, F32),
        scratch_shapes=[pltpu.VMEM((tm, D_MODEL), BF16), pltpu.VMEM((tm, D_MODEL), F32)],
        compiler_params=_cparams("parallel", "arbitrary"),
        name="ffn",
    )(x2, gpre, wup, wdown, gpost)


def _hgrn_inproj_kernel(x_ref, gpre_ref, w_ref, lbl_ref, q_ref, k_ref, g_ref, v_ref, sg_ref, *, layer):
    h = _rms(x_ref[...], gpre_ref[...]).astype(BF16)
    logits = lbl_ref[...]
    e = jnp.exp(logits - jnp.max(logits, axis=0, keepdims=True))
    sm = e / jnp.sum(e, axis=0, keepdims=True)
    lb = jnp.sum(sm[1:layer + 1, :], axis=0, keepdims=True)
    zq = jnp.dot(h, w_ref[:, 0:D_MODEL], preferred_element_type=F32)
    q_ref[...] = (zq * _sigmoid(zq)).astype(BF16)
    zf = jnp.dot(h, w_ref[:, D_MODEL:2 * D_MODEL], preferred_element_type=F32)
    f = lb + (1.0 - lb) * _sigmoid(zf)
    k_ref[...] = (1.0 - f).astype(BF16)
    g_ref[...] = jnp.log(f)
    v_ref[...] = jnp.dot(h, w_ref[:, 2 * D_MODEL:3 * D_MODEL], preferred_element_type=F32).astype(BF16)
    zg = jnp.dot(h, w_ref[:, 3 * D_MODEL:4 * D_MODEL], preferred_element_type=F32)
    sg_ref[...] = (zg * _sigmoid(zg)).astype(BF16)


def _hgrn_inproj(x2, gpre, w, lb_logits, layer):
    m = x2.shape[0]
    tm = _row_tile(m, 512)
    row = pl.BlockSpec((tm, D_MODEL), lambda i: (i, 0))
    wide = lambda dt: jax.ShapeDtypeStruct((m, D_MODEL), dt)
    return pl.pallas_call(
        functools.partial(_hgrn_inproj_kernel, layer=layer),
        grid=(m // tm,),
        in_specs=[row, _resident((1, D_MODEL)), _resident(w.shape), _resident(lb_logits.shape)],
        out_specs=[row] * 5,
        out_shape=[wide(BF16), wide(BF16), wide(F32), wide(BF16), wide(BF16)],
        compiler_params=_cparams("parallel"),
        name="hgrn_inproj",
    )(x2, gpre, w, lb_logits)


SAFE_CHUNK_DECAY = -60.0


def _hgrn_rec_kernel(q_ref, k_ref, v_ref, g_ref, s0_ref, o_ref, sout_ref, st_sc, *, chunk, n_chunks):
    t_idx = pl.program_id(2)

    @pl.when(t_idx == 0)
    def _():
        st_sc[...] = s0_ref[0, 0].T

    row = lax.broadcasted_iota(jnp.int32, (chunk, LANES), 0)
    tril = (lax.broadcasted_iota(jnp.int32, (chunk, chunk), 0)
            >= lax.broadcasted_iota(jnp.int32, (chunk, chunk), 1))
    nt = (((1,), (1,)), ((), ()))
    tn = (((0,), (0,)), ((), ()))

    def one_chunk(c, carry):
        r0 = pl.multiple_of(c * chunk, chunk)
        q = q_ref[0, pl.ds(r0, chunk), :].astype(F32)
        k = k_ref[0, pl.ds(r0, chunk), :].astype(F32)
        v = v_ref[0, pl.ds(r0, chunk), :]
        b = g_ref[0, pl.ds(r0, chunk), :]
        sh = 1
        while sh < chunk:
            b = b + jnp.where(row >= sh, pltpu.roll(b, sh, axis=0), 0.0)
            sh *= 2
        bl = b[chunk - 1:chunk, :]
        st = st_sc[...]
        qd = q * jnp.exp(b)
        o_inter = lax.dot_general(qd.astype(BF16), st.astype(BF16), nt, preferred_element_type=F32)
        kd = (k * jnp.exp(bl - b)).astype(BF16)
        st_sc[...] = st * jnp.exp(bl) + lax.dot_general(v, kd, tn, preferred_element_type=F32)

        def fast(_):
            ku = (k * jnp.exp(-b)).astype(BF16)
            a = lax.dot_general(qd.astype(BF16), ku, nt, preferred_element_type=F32)
            a = jnp.where(tril, a, 0.0)
            return jnp.dot(a.astype(BF16), v, preferred_element_type=F32)

        def exact(_):
            vf = v.astype(F32)

            def shift(d, acc):
                valid = row >= d
                diff = jnp.where(valid, b - pltpu.roll(b, d, axis=0), 0.0)
                w = jnp.sum(q * pltpu.roll(k, d, axis=0) * jnp.exp(jnp.minimum(diff, 0.0)),
                            axis=-1, keepdims=True)
                return acc + jnp.where(valid, w * pltpu.roll(vf, d, axis=0), 0.0)

            return lax.fori_loop(0, chunk, shift, jnp.zeros((chunk, LANES), F32))

        o_intra = lax.cond(jnp.min(bl) >= SAFE_CHUNK_DECAY, fast, exact, 0)
        o_ref[0, pl.ds(r0, chunk), :] = (o_inter + o_intra).astype(BF16)
        return carry

    lax.fori_loop(0, n_chunks, one_chunk, 0)

    @pl.when(t_idx == pl.num_programs(2) - 1)
    def _():
        sout_ref[0, 0] = st_sc[...].T


def _hgrn_recurrence(q, k, v, g, s0):
    b, t, _ = q.shape
    tb = _row_tile(t, 512)
    chunk = min(64, tb)
    blk = pl.BlockSpec((1, tb, LANES), lambda bi, h, ti: (bi, ti, h))
    sblk = pl.BlockSpec((1, 1, HGRN_KEY_DIM, HGRN_KEY_DIM), lambda bi, h, ti: (bi, h, 0, 0))
    return pl.pallas_call(
        functools.partial(_hgrn_rec_kernel, chunk=chunk, n_chunks=tb // chunk),
        grid=(b, HGRN_HEADS, t // tb),
        in_specs=[blk, blk, blk, blk, sblk],
        out_specs=[blk, sblk],
        out_shape=[jax.ShapeDtypeStruct(q.shape, BF16), jax.ShapeDtypeStruct(s0.shape, F32)],
        scratch_shapes=[pltpu.VMEM((HGRN_KEY_DIM, HGRN_KEY_DIM), F32)],
        compiler_params=_cparams("parallel", "parallel", "arbitrary"),
        name="hgrn_recurrence",
    )(q, k, v, g, s0)


def _prep_weights(fox_w_in, fox_b_f, fox_q_norm, fox_k_norm, fox_w_out, hgrn_w_in, hgrn_w_out,
                  ffn_w_up, ffn_w_down):
    n_attn = fox_w_in.shape[0]
    w_f = jnp.zeros((n_attn, D_MODEL, LANES), F32).at[:, :, :FOX_HEADS].set(fox_w_in[:, :, 4 * D_MODEL:])
    b_f = jnp.zeros((n_attn, 1, LANES), F32).at[:, 0, :FOX_HEADS].set(fox_b_f)
    head_of_lane = jnp.arange(D_MODEL) // FOX_HEAD_DIM
    sel = (head_of_lane[:, None] == jnp.arange(LANES)[None, :]).astype(BF16)
    return dict(
        fox_w_main=fox_w_in[:, :, :4 * D_MODEL].astype(BF16),
        fox_w_f=w_f.astype(BF16),
        fox_b_f=b_f,
        fox_gq=jnp.tile(fox_q_norm, (1, FOX_HEADS))[:, None, :],
        fox_gk=jnp.tile(fox_k_norm, (1, FOX_HEADS))[:, None, :],
        sel=sel, selt=sel.T,
        fox_w_out=fox_w_out.astype(BF16),
        hgrn_w_in=hgrn_w_in.astype(BF16),
        hgrn_w_out=hgrn_w_out.astype(BF16),
        ffn_w_up=ffn_w_up.astype(BF16),
        ffn_w_down=ffn_w_down.astype(BF16),
    )


def _head_rows(c, t0, t1):
    ct = jnp.swapaxes(c[:, t0:t1, :FOX_HEADS], 1, 2)
    return ct.reshape(c.shape[0], HEAD_PAIRS, 2, t1 - t0)


def _trunk(x, past_k, past_v, past_logf, past_s, w, hgrn_lb_logits, hgrn_out_norm,
           pre_mix_norm, post_mix_norm, pre_ffn_norm, post_ffn_norm):
    b, t, _ = x.shape
    m = b * t
    x2 = x.reshape(m, D_MODEL)
    vec = lambda a: a.reshape(1, D_MODEL)
    new_k, new_v, new_logf, new_s = [], [], [], []
    for layer in range(DEPTH):
        j = layer // 2
        if layer % 2 == 0:
            q16, k32, k16, v32, v16, gate16, logf, logfp = _fox_inproj(
                x2, vec(pre_mix_norm[layer]), w["fox_w_main"][j], w["fox_w_f"][j], w["fox_b_f"][j],
                w["fox_gq"][j], w["fox_gk"][j], w["sel"], w["selt"])
            as3 = lambda a: a.reshape(b, t, a.shape[-1])
            if past_k is None:
                c = _cumsum_time(as3(logfp))
                o = _fox_attention(as3(q16), as3(k16), as3(v16), as3(gate16), _head_rows(c, 0, t))
            else:
                p = past_k.shape[2]
                tt = 512
                total = -(-(p + t) // tt) * tt
                lf_all = jnp.zeros((b, total, LANES), F32)
                lf_all = lf_all.at[:, :p, :FOX_HEADS].set(past_logf[j]).at[:, p:p + t, :].set(as3(logfp))
                c = _cumsum_time(lf_all)
                o = _fox_decode_attention(
                    as3(q16), past_k[j].reshape(b, p, D_MODEL), past_v[j].reshape(b, p, D_MODEL),
                    as3(k16), as3(v16), as3(gate16), _head_rows(c, 0, p), _head_rows(c, p, p + t))
            x2 = _fox_outproj(o.reshape(m, D_MODEL), x2, w["fox_w_out"][j], vec(post_mix_norm[layer]))
            new_k.append(k32.reshape(b, t, FOX_HEADS, FOX_HEAD_DIM))
            new_v.append(v32.reshape(b, t, FOX_HEADS, FOX_HEAD_DIM))
            new_logf.append(logf.reshape(b, t, FOX_HEADS))
        else:
            q16, k16, g32, v16, sg16 = _hgrn_inproj(x2, vec(pre_mix_norm[layer]), w["hgrn_w_in"][j],
                                                    hgrn_lb_logits, layer)
            as3 = lambda a: a.reshape(b, t, D_MODEL)
            s0 = (jnp.zeros((b, HGRN_HEADS, HGRN_KEY_DIM, HGRN_KEY_DIM), F32) if past_s is None else past_s[j])
            o, s = _hgrn_recurrence(as3(q16), as3(k16), as3(v16), as3(g32), s0)
            x2 = _hgrn_outproj(o.reshape(m, D_MODEL), sg16, x2, vec(hgrn_out_norm[j]), w["hgrn_w_out"][j],
                               vec(post_mix_norm[layer]))
            new_s.append(s)
        x2 = _ffn(x2, vec(pre_ffn_norm[layer]), w["ffn_w_up"][layer], w["ffn_w_down"][layer],
                  vec(post_ffn_norm[layer]))
    return (x2.reshape(b, t, D_MODEL), jnp.stack(new_k), jnp.stack(new_v), jnp.stack(new_logf),
            jnp.stack(new_s))


def kernel(x_prompt, x_sample, cache_k, cache_v, cache_logf, state_s, fox_w_in, fox_b_f, fox_q_norm,
           fox_k_norm, fox_w_out, hgrn_w_in, hgrn_lb_logits, hgrn_out_norm, hgrn_w_out, pre_mix_norm,
           post_mix_norm, pre_ffn_norm, post_ffn_norm, ffn_w_up, ffn_w_down):
    w = _prep_weights(fox_w_in, fox_b_f, fox_q_norm, fox_k_norm, fox_w_out, hgrn_w_in, hgrn_w_out,
                      ffn_w_up, ffn_w_down)
    norms = (hgrn_lb_logits, hgrn_out_norm, pre_mix_norm, post_mix_norm, pre_ffn_norm, post_ffn_norm)
    y_p, k_p, v_p, lf_p, s_p = _trunk(x_prompt, None, None, None, None, w, *norms)
    y_s, k_s, v_s, lf_s, s_s = _trunk(x_sample, cache_k, cache_v, cache_logf, state_s, w, *norms)
    return (y_p, y_s, k_p, v_p, lf_p, s_p, k_s, v_s, lf_s, s_s)
```
